```python
import math
import jax, jax.numpy as jnp
from jax import lax
import numpy as np

D_MODEL = 1024
BATCH = 16
SEQ = 2048
DEPTH = 1

N_META = 16
GRID_W = 64
Q_BLOCK = 128
HEAD_DIM = 64
ATTN_WIDTH = D_MODEL // 2
ATTN_HEADS = ATTN_WIDTH // HEAD_DIM
KV_HEADS = 2
KV_WIDTH = KV_HEADS * HEAD_DIM
FOURIER_WIDTH = D_MODEL - ATTN_WIDTH
FOURIER_GROUP_DIM = 64
FOURIER_GROUPS = FOURIER_WIDTH // FOURIER_GROUP_DIM
MIX_WIDTH = ATTN_WIDTH + FOURIER_WIDTH
IN_WIDTH = ATTN_WIDTH + 2 * KV_WIDTH + FOURIER_WIDTH
D_FF = 2816
ROPE_THETA = 10000.0
ROPE_PAIRS = HEAD_DIM // 2
ROPE_AXIS_PAIRS = ROPE_PAIRS // 2
RMS_EPS = 1e-6
LN_EPS = 1e-5
DEEPNORM_ALPHA = (2.0 * DEPTH) ** 0.25
DEEPNORM_BETA = (8.0 * DEPTH) ** -0.25

kernel_name = "hybrid_gqa_fourier_macaron_deepnorm_encoder"


def layer_norm(x, g, b):
    xf = x.astype(jnp.float32)
    mu = jnp.mean(xf, axis=-1, keepdims=True)
    var = jnp.mean(jnp.square(xf - mu), axis=-1, keepdims=True)
    y = (xf - mu) * lax.rsqrt(var + LN_EPS) * g.astype(jnp.float32) + b.astype(jnp.float32)
    return y.astype(x.dtype)


def rms_norm(x, g):
    xf = x.astype(jnp.float32)
    y = xf * lax.rsqrt(jnp.mean(jnp.square(xf), axis=-1, keepdims=True) + RMS_EPS) * g.astype(jnp.float32)
    return y.astype(x.dtype)


def swiglu(x, w_gate, w_up, w_down):
    return (jax.nn.silu(x @ w_gate) * (x @ w_up)) @ w_down


def axial_rope_tables(n_tokens):
    n_rows = n_tokens // GRID_W
    row = jnp.repeat(jnp.arange(n_rows, dtype=jnp.float32), GRID_W)
    col = jnp.tile(jnp.arange(GRID_W, dtype=jnp.float32), n_rows)
    inv_freq = ROPE_THETA ** (-jnp.arange(ROPE_AXIS_PAIRS, dtype=jnp.float32) / ROPE_AXIS_PAIRS)
    ang = jnp.concatenate([row[:, None] * inv_freq, col[:, None] * inv_freq], axis=-1)
    ang = jnp.concatenate([jnp.zeros((N_META, ROPE_PAIRS), jnp.float32), ang], axis=0)
    return jnp.cos(ang), jnp.sin(ang)


def apply_rope(x, cos, sin):
    B, L, H, D = x.shape
    xp = x.astype(jnp.float32).reshape(B, L, H, D // 2, 2)
    x0, x1 = xp[..., 0], xp[..., 1]
    c = cos[None, :, None, :]
    s = sin[None, :, None, :]
    out = jnp.stack([x0 * c - x1 * s, x0 * s + x1 * c], axis=-1)
    return out.reshape(B, L, H, D).astype(x.dtype)


def gqa_attention(q, k, v):
    B, L, H, Dh = q.shape
    G = k.shape[2]
    R = H // G
    q = q.reshape(B, L, G, R, Dh)
    scale = Dh ** -0.5

    def attend(qb):
        s = jnp.einsum('bqgrd,bkgd->bgrqk', qb, k, preferred_element_type=jnp.float32) * scale
        p = jax.nn.softmax(s, axis=-1).astype(v.dtype)
        return jnp.einsum('bgrqk,bkgd->bqgrd', p, v)

    out_meta = attend(q[:, :N_META]).reshape(B, N_META, H * Dh)
    n_real = L - N_META
    n_blocks = n_real // Q_BLOCK
    q_blocks = q[:, N_META:].reshape(B, n_blocks, Q_BLOCK, G, R, Dh).transpose(1, 0, 2, 3, 4, 5)
    out_real = lax.map(attend, q_blocks)
    out_real = out_real.transpose(1, 0, 2, 3, 4, 5).reshape(B, n_real, H * Dh)
    return jnp.concatenate([out_meta, out_real], axis=1)


def fourier_mix(u):
    y = jnp.fft.fft2(u.astype(jnp.float32), axes=(1, 3), norm='ortho').real
    return y.astype(u.dtype)


def token_mixers(h, cos, sin, w_in, q_norm_g, k_norm_g, attn_out_g, fourier_out_g, w_out):
    B, L, _ = h.shape
    z = h @ w_in
    q, k, v, f = jnp.split(z, [ATTN_WIDTH, ATTN_WIDTH + KV_WIDTH, ATTN_WIDTH + 2 * KV_WIDTH], axis=-1)
    q = apply_rope(rms_norm(q.reshape(B, L, ATTN_HEADS, HEAD_DIM), q_norm_g), cos, sin)
    k = apply_rope(rms_norm(k.reshape(B, L, KV_HEADS, HEAD_DIM), k_norm_g), cos, sin)
    v = v.reshape(B, L, KV_HEADS, HEAD_DIM)
    attn = gqa_attention(q, k, v)
    four = fourier_mix(f.reshape(B, L, FOURIER_GROUPS, FOURIER_GROUP_DIM)).reshape(B, L, FOURIER_WIDTH)
    merged = jnp.concatenate([rms_norm(attn, attn_out_g), rms_norm(four, fourier_out_g)], axis=-1)
    return merged @ w_out


def hybrid_layer(h, cos, sin,
                 ff1_gate, ff1_up, ff1_down, ln1_g, ln1_b,
                 w_in, q_norm_g, k_norm_g, attn_out_g, fourier_out_g, w_out, ln2_g, ln2_b,
                 ff2_gate, ff2_up, ff2_down, ln3_g, ln3_b):
    a = DEEPNORM_ALPHA
    h = layer_norm(a * h + 0.5 * swiglu(h, ff1_gate, ff1_up, ff1_down), ln1_g, ln1_b)
    mix = token_mixers(h, cos, sin, w_in, q_norm_g, k_norm_g, attn_out_g, fourier_out_g, w_out)
    h = layer_norm(a * h + mix, ln2_g, ln2_b)
    h = layer_norm(a * h + 0.5 * swiglu(h, ff2_gate, ff2_up, ff2_down), ln3_g, ln3_b)
    return h


def setup_inputs(seed: int = 0) -> dict:
    key = jax.random.key(seed)
    ks = jax.random.split(key, 24)
    nrm = lambda k, shape: jax.random.normal(k, shape, jnp.float32)
    gain = lambda k, shape: 1.0 + 0.02 * nrm(k, shape)
    bias = lambda k, shape: 0.02 * nrm(k, shape)
    beta = DEEPNORM_BETA
    return {
        "x": nrm(ks[0], (BATCH, SEQ, D_MODEL)),
        "meta_tokens": nrm(ks[1], (N_META, D_MODEL)),
        "ln_emb_g": gain(ks[2], (D_MODEL,)),
        "ln_emb_b": bias(ks[3], (D_MODEL,)),
        "ff1_gate": nrm(ks[4], (DEPTH, D_MODEL, D_FF)) * D_MODEL ** -0.5,
        "ff1_up": nrm(ks[5], (DEPTH, D_MODEL, D_FF)) * D_MODEL ** -0.5,
        "ff1_down": nrm(ks[6], (DEPTH, D_FF, D_MODEL)) * (D_FF ** -0.5 * beta),
        "ln1_g": gain(ks[7], (DEPTH, D_MODEL)),
        "ln1_b": bias(ks[8], (DEPTH, D_MODEL)),
        "w_in": nrm(ks[9], (DEPTH, D_MODEL, IN_WIDTH)) * D_MODEL ** -0.5,
        "q_norm_g": gain(ks[10], (DEPTH, HEAD_DIM)),
        "k_norm_g": gain(ks[11], (DEPTH, HEAD_DIM)),
        "attn_out_g": gain(ks[12], (DEPTH, ATTN_WIDTH)),
        "fourier_out_g": gain(ks[13], (DEPTH, FOURIER_WIDTH)),
        "w_out": nrm(ks[14], (DEPTH, MIX_WIDTH, D_MODEL)) * (MIX_WIDTH ** -0.5 * beta),
        "ln2_g": gain(ks[15], (DEPTH, D_MODEL)),
        "ln2_b": bias(ks[16], (DEPTH, D_MODEL)),
        "ff2_gate": nrm(ks[17], (DEPTH, D_MODEL, D_FF)) * D_MODEL ** -0.5,
        "ff2_up": nrm(ks[18], (DEPTH, D_MODEL, D_FF)) * D_MODEL ** -0.5,
        "ff2_down": nrm(ks[19], (DEPTH, D_FF, D_MODEL)) * (D_FF ** -0.5 * beta),
        "ln3_g": gain(ks[20], (DEPTH, D_MODEL)),
        "ln3_b": bias(ks[21], (DEPTH, D_MODEL)),
    }


def reference(x, meta_tokens, ln_emb_g, ln_emb_b,
              ff1_gate, ff1_up, ff1_down, ln1_g, ln1_b,
              w_in, q_norm_g, k_norm_g, attn_out_g, fourier_out_g, w_out, ln2_g, ln2_b,
              ff2_gate, ff2_up, ff2_down, ln3_g, ln3_b):
    B, S, D = x.shape
    meta = jnp.broadcast_to(meta_tokens.astype(x.dtype)[None], (B, N_META, D))
    h = jnp.concatenate([meta, x], axis=1)
    h = layer_norm(h, ln_emb_g, ln_emb_b)
    cos, sin = axial_rope_tables(S)
    for l in range(DEPTH):
        h = hybrid_layer(h, cos, sin,
                         ff1_gate[l], ff1_up[l], ff1_down[l], ln1_g[l], ln1_b[l],
                         w_in[l], q_norm_g[l], k_norm_g[l], attn_out_g[l], fourier_out_g[l], w_out[l],
                         ln2_g[l], ln2_b[l],
                         ff2_gate[l], ff2_up[l], ff2_down[l], ln3_g[l], ln3_b[l])
    return h[:, N_META:]
```

```python
import functools
import math

import numpy as np
import jax
import jax.numpy as jnp
from jax import lax
from jax.experimental import pallas as pl
from jax.experimental.pallas import tpu as pltpu

F32 = jnp.float32
BF16 = jnp.bfloat16

D_MODEL = 1024
N_META = 16
GRID_W = 64
HEAD_DIM = 64
ATTN_WIDTH = 512
KV_WIDTH = 128
FOURIER_WIDTH = 512
FOURIER_GROUP_DIM = 64
D_FF = 2816
ROPE_THETA = 10000.0
RMS_EPS = 1e-6
LN_EPS = 1e-5
DEPTH = 1
ALPHA = (2.0 * DEPTH) ** 0.25

LANES = 128
VMEM_LIMIT = 56 * 1024 * 1024

TM = 512
TQ = 512
TR = 1024
FF_CHUNKS = ((0, 1024), (1024, 2048), (2048, 2816))
MASKED = -1e30


def _dot(a, b):
    return jnp.dot(a, b, preferred_element_type=F32)


def _dot_nt(a, b):
    return lax.dot_general(a, b, (((1,), (1,)), ((), ())), preferred_element_type=F32)


def _layer_norm(x, g, b):
    mu = jnp.mean(x, axis=-1, keepdims=True)
    xc = x - mu
    var = jnp.mean(xc * xc, axis=-1, keepdims=True)
    return xc * lax.rsqrt(var + LN_EPS) * g + b


def _swiglu(hb, wg_ref, wu_ref, wd_ref):
    acc = None
    for c0, c1 in FF_CHUNKS:
        g = _dot(hb, wg_ref[:, c0:c1])
        u = _dot(hb, wu_ref[:, c0:c1])
        hid = (g * jax.nn.sigmoid(g) * u).astype(BF16)
        d = _dot(hid, wd_ref[c0:c1, :])
        acc = d if acc is None else acc + d
    return acc


def _front_body(x_ref, cos_ref, sin_ref, lneg_ref, lneb_ref, wg_ref, wu_ref, wd_ref,
                l1g_ref, l1b_ref, win_ref, qg_ref, kg_ref,
                h1_ref, q_ref, k_ref, v_ref, u_ref):
    h0 = _layer_norm(x_ref[...], lneg_ref[...], lneb_ref[...])
    f = _swiglu(h0.astype(BF16), wg_ref, wu_ref, wd_ref)
    h1 = _layer_norm(ALPHA * h0 + 0.5 * f, l1g_ref[...], l1b_ref[...])
    h1_ref[...] = h1
    z = _dot(h1.astype(BF16), win_ref[...])

    cos = cos_ref[...]
    sin = sin_ref[...]
    lane = lax.broadcasted_iota(jnp.int32, (1, LANES), 1)
    lo = lane < HEAD_DIM
    even = (lane & 1) == 0

    def norm_rope(zc, g):
        sq = zc * zc
        s_lo = jnp.sum(jnp.where(lo, sq, 0.0), axis=-1, keepdims=True)
        s_hi = jnp.sum(jnp.where(lo, 0.0, sq), axis=-1, keepdims=True)
        ms = jnp.where(lo, s_lo, s_hi) * (1.0 / HEAD_DIM)
        y = zc * lax.rsqrt(ms + RMS_EPS) * g
        nxt = pltpu.roll(y, LANES - 1, 1)
        prv = pltpu.roll(y, 1, 1)
        return y * cos + jnp.where(even, nxt, prv) * sin

    qg = qg_ref[...]
    for p in range(ATTN_WIDTH // LANES):
        c = slice(p * LANES, (p + 1) * LANES)
        q_ref[:, c] = (norm_rope(z[:, c], qg) * (HEAD_DIM ** -0.5)).astype(BF16)
    k_ref[...] = norm_rope(z[:, ATTN_WIDTH:ATTN_WIDTH + KV_WIDTH], kg_ref[...]).astype(BF16)
    v_ref[...] = z[:, ATTN_WIDTH + KV_WIDTH:ATTN_WIDTH + 2 * KV_WIDTH].astype(BF16)
    u_ref[...] = z[:, ATTN_WIDTH + 2 * KV_WIDTH:].astype(BF16)


def _const_spec(shape):
    return pl.BlockSpec(shape, lambda *_: (0,) * len(shape), pipeline_mode=pl.Buffered(1))


def _front(x2, cos_t, sin_t, lneg, lneb, wg, wu, wd, l1g, l1b, win, qg, kg, *, tm, name):
    n = x2.shape[0]
    n_pos = cos_t.shape[0] // tm
    in_w = win.shape[1]
    row = lambda w: pl.BlockSpec((tm, w), lambda i: (i, 0))
    pos = pl.BlockSpec((tm, LANES), lambda i: (i % n_pos, 0))
    out_shape = (
        jax.ShapeDtypeStruct((n, D_MODEL), F32),
        jax.ShapeDtypeStruct((n, ATTN_WIDTH), BF16),
        jax.ShapeDtypeStruct((n, KV_WIDTH), BF16),
        jax.ShapeDtypeStruct((n, KV_WIDTH), BF16),
        jax.ShapeDtypeStruct((n, FOURIER_WIDTH), BF16),
    )
    return pl.pallas_call(
        _front_body,
        out_shape=out_shape,
        grid=(n // tm,),
        in_specs=[row(D_MODEL), pos, pos,
                  _const_spec((1, D_MODEL)), _const_spec((1, D_MODEL)),
                  _const_spec((D_MODEL, D_FF)), _const_spec((D_MODEL, D_FF)), _const_spec((D_FF, D_MODEL)),
                  _const_spec((1, D_MODEL)), _const_spec((1, D_MODEL)),
                  _const_spec((D_MODEL, in_w)),
                  _const_spec((1, LANES)), _const_spec((1, LANES))],
        out_specs=(row(D_MODEL), row(ATTN_WIDTH), row(KV_WIDTH), row(KV_WIDTH), row(FOURIER_WIDTH)),
        compiler_params=pltpu.CompilerParams(dimension_semantics=("arbitrary",),
                                             vmem_limit_bytes=VMEM_LIMIT),
        name=name,
    )(x2, cos_t, sin_t, lneg, lneb, wg, wu, wd, l1g, l1b, win, qg, kg)


def _attn_body(q_ref, k_ref, v_ref, km_ref, vm_ref, g_ref, o_ref):
    lane = lax.broadcasted_iota(jnp.int32, (1, LANES), 1)
    lo = lane < HEAD_DIM
    k = k_ref[...]
    km = km_ref[...]
    v = v_ref[...]
    vm = vm_ref[...]
    zk = jnp.zeros_like(k)
    zm = jnp.zeros_like(km)
    k_sel = (jnp.where(lo, k, zk), jnp.where(lo, zk, k))
    km_sel = (jnp.where(lo, km, zm), jnp.where(lo, zm, km))
    meta_bias = jnp.where(lane < N_META, 0.0, MASKED).astype(F32)

    outs = []
    ssq = None
    for p in range(ATTN_WIDTH // LANES):
        q2 = q_ref[:, p * LANES:(p + 1) * LANES]
        per_group = []
        for g in range(2):
            s = _dot_nt(q2, k_sel[g])
            sm = _dot_nt(q2, km_sel[g]) + meta_bias
            m = jnp.maximum(jnp.max(s, axis=-1, keepdims=True), jnp.max(sm, axis=-1, keepdims=True))
            e = jnp.exp(s - m)
            em = jnp.exp(sm - m)
            denom = jnp.sum(e, axis=-1, keepdims=True) + jnp.sum(em, axis=-1, keepdims=True)
            o = _dot(e.astype(BF16), v) + _dot(em.astype(BF16), vm)
            per_group.append(o / denom)
        op = jnp.where(lo, per_group[0], per_group[1])
        sq = jnp.sum(op * op, axis=-1, keepdims=True)
        ssq = sq if ssq is None else ssq + sq
        outs.append(op)
    rinv = lax.rsqrt(ssq * (1.0 / ATTN_WIDTH) + RMS_EPS)
    for p, op in enumerate(outs):
        c = slice(p * LANES, (p + 1) * LANES)
        o_ref[:, c] = (op * rinv * g_ref[:, c]).astype(BF16)


def _attention(q, k, v, km, vm, g, *, batch, seq):
    nq = seq // TQ
    return pl.pallas_call(
        _attn_body,
        out_shape=jax.ShapeDtypeStruct((batch * seq, ATTN_WIDTH), BF16),
        grid=(batch, nq),
        in_specs=[pl.BlockSpec((TQ, ATTN_WIDTH), lambda b, i: (b * nq + i, 0)),
                  pl.BlockSpec((seq, KV_WIDTH), lambda b, i: (b, 0)),
                  pl.BlockSpec((seq, KV_WIDTH), lambda b, i: (b, 0)),
                  _const_spec((LANES, KV_WIDTH)), _const_spec((LANES, KV_WIDTH)),
                  _const_spec((1, ATTN_WIDTH))],
        out_specs=pl.BlockSpec((TQ, ATTN_WIDTH), lambda b, i: (b * nq + i, 0)),
        compiler_params=pltpu.CompilerParams(dimension_semantics=("arbitrary", "arbitrary"),
                                             vmem_limit_bytes=VMEM_LIMIT),
        name="attention",
    )(q, k, v, km, vm, g)


def _fourier_body(ct_ref, st_ref, cm_ref, sm_ref, u_ref, um_ref, c64_ref, s64_ref, g_ref,
                  o_ref, ymeta_ref):
    @pl.when(pl.program_id(1) == 0)
    def _():
        um = um_ref[...]
        tcm = _dot(cm_ref[...], um).astype(BF16)
        tsm = _dot(sm_ref[...], um).astype(BF16)
        ymeta_ref[...] = _dot(tcm, c64_ref[...]) + _dot(tsm, s64_ref[...])

    u = u_ref[...]
    tc = _dot(ct_ref[...], u).astype(BF16)
    ts = _dot(st_ref[...], u).astype(BF16)
    y = _dot(tc, c64_ref[...]) + _dot(ts, s64_ref[...]) + ymeta_ref[...]
    rinv = lax.rsqrt(jnp.mean(y * y, axis=-1, keepdims=True) + RMS_EPS)
    o_ref[...] = (y * rinv * g_ref[...]).astype(BF16)


def _fourier(ct, st, cm, sm, u, um, c64, s64, g, *, batch, seq):
    nr = seq // TR
    return pl.pallas_call(
        _fourier_body,
        out_shape=jax.ShapeDtypeStruct((batch * seq, FOURIER_WIDTH), BF16),
        grid=(nr, batch),
        in_specs=[pl.BlockSpec((TR, seq), lambda r, b: (r, 0)),
                  pl.BlockSpec((TR, seq), lambda r, b: (r, 0)),
                  pl.BlockSpec((TR, LANES), lambda r, b: (r, 0)),
                  pl.BlockSpec((TR, LANES), lambda r, b: (r, 0)),
                  pl.BlockSpec((seq, FOURIER_WIDTH), lambda r, b: (b, 0)),
                  _const_spec((LANES, FOURIER_WIDTH)),
                  _const_spec((FOURIER_WIDTH, FOURIER_WIDTH)), _const_spec((FOURIER_WIDTH, FOURIER_WIDTH)),
                  _const_spec((1, FOURIER_WIDTH))],
        out_specs=pl.BlockSpec((TR, FOURIER_WIDTH), lambda r, b: (b * nr + r, 0)),
        scratch_shapes=[pltpu.VMEM((TR, FOURIER_WIDTH), F32)],
        compiler_params=pltpu.CompilerParams(dimension_semantics=("arbitrary", "arbitrary"),
                                             vmem_limit_bytes=VMEM_LIMIT),
        name="fourier",
    )(ct, st, cm, sm, u, um, c64, s64, g)


def _back_body(a_ref, f_ref, h1_ref, woa_ref, wof_ref, l2g_ref, l2b_ref, wg_ref, wu_ref, wd_ref,
               l3g_ref, l3b_ref, o_ref):
    mix = _dot(a_ref[...], woa_ref[...]) + _dot(f_ref[...], wof_ref[...])
    h2 = _layer_norm(ALPHA * h1_ref[...] + mix, l2g_ref[...], l2b_ref[...])
    f = _swiglu(h2.astype(BF16), wg_ref, wu_ref, wd_ref)
    o_ref[...] = _layer_norm(ALPHA * h2 + 0.5 * f, l3g_ref[...], l3b_ref[...])


def _back(a, f, h1, woa, wof, l2g, l2b, wg, wu, wd, l3g, l3b):
    n = h1.shape[0]
    row = lambda w: pl.BlockSpec((TM, w), lambda i: (i, 0))
    return pl.pallas_call(
        _back_body,
        out_shape=jax.ShapeDtypeStruct((n, D_MODEL), F32),
        grid=(n // TM,),
        in_specs=[row(ATTN_WIDTH), row(FOURIER_WIDTH), row(D_MODEL),
                  _const_spec((ATTN_WIDTH, D_MODEL)), _const_spec((FOURIER_WIDTH, D_MODEL)),
                  _const_spec((1, D_MODEL)), _const_spec((1, D_MODEL)),
                  _const_spec((D_MODEL, D_FF)), _const_spec((D_MODEL, D_FF)), _const_spec((D_FF, D_MODEL)),
                  _const_spec((1, D_MODEL)), _const_spec((1, D_MODEL))],
        out_specs=row(D_MODEL),
        compiler_params=pltpu.CompilerParams(dimension_semantics=("arbitrary",),
                                             vmem_limit_bytes=VMEM_LIMIT),
        name="back",
    )(a, f, h1, woa, wof, l2g, l2b, wg, wu, wd, l3g, l3b)


@functools.lru_cache(maxsize=None)
def _rope_tables(seq):
    t = np.arange(seq)
    row = (t // GRID_W).astype(np.float64)
    col = (t % GRID_W).astype(np.float64)
    n_axis = HEAD_DIM // 4
    inv_freq = ROPE_THETA ** (-np.arange(n_axis, dtype=np.float64) / n_axis)
    ang = np.concatenate([row[:, None] * inv_freq, col[:, None] * inv_freq], axis=-1)
    cos = np.repeat(np.cos(ang), 2, axis=-1)
    sin = np.repeat(np.sin(ang), 2, axis=-1) * np.tile([-1.0, 1.0], HEAD_DIM // 2)
    return (np.tile(cos, (1, 2)).astype(np.float32), np.tile(sin, (1, 2)).astype(np.float32))


@functools.lru_cache(maxsize=None)
def _dft_tables(seq):
    L = seq + N_META
    m = np.arange(N_META, L, dtype=np.int64)[:, None]
    l = np.arange(L, dtype=np.int64)[None, :]
    ang = (2.0 * np.pi / L) * ((m * l) % L).astype(np.float64)
    c = (np.cos(ang) / math.sqrt(L)).astype(np.float32)
    s = (-np.sin(ang) / math.sqrt(L)).astype(np.float32)
    pad = ((0, 0), (0, LANES - N_META))
    return (c[:, N_META:], s[:, N_META:], np.pad(c[:, :N_META], pad), np.pad(s[:, :N_META], pad))


@functools.lru_cache(maxsize=None)
def _channel_dft_tables():
    n = FOURIER_GROUP_DIM
    idx = np.arange(n, dtype=np.int64)
    ang = (2.0 * np.pi / n) * ((idx[:, None] * idx[None, :]) % n).astype(np.float64)
    eye = np.eye(FOURIER_WIDTH // n)
    c = np.kron(eye, np.cos(ang) / math.sqrt(n)).astype(np.float32)
    s = np.kron(eye, np.sin(ang) / math.sqrt(n)).astype(np.float32)
    return c, s


def kernel(x, meta_tokens, ln_emb_g, ln_emb_b, ff1_gate, ff1_up, ff1_down, ln1_g, ln1_b, w_in, q_norm_g,
           k_norm_g, attn_out_g, fourier_out_g, w_out, ln2_g, ln2_b, ff2_gate, ff2_up, ff2_down, ln3_g, ln3_b):
    B, S, D = x.shape
    assert D == D_MODEL and S % TM == 0 and S % TQ == 0 and S % TR == 0
    assert w_in.shape[0] == DEPTH == 1
    vec = lambda a: a.reshape(1, -1).astype(F32)
    bf = lambda a: a.astype(BF16)

    n_rep = (ATTN_WIDTH // HEAD_DIM) // 2
    head_order = np.stack([np.arange(n_rep), np.arange(n_rep) + n_rep], axis=1).reshape(-1)
    q_cols = (head_order[:, None] * HEAD_DIM + np.arange(HEAD_DIM)[None, :]).reshape(-1)
    w_in0 = w_in[0]
    win = bf(jnp.concatenate([w_in0[:, q_cols], w_in0[:, ATTN_WIDTH:]], axis=1))
    w_out0 = w_out[0]
    woa = bf(w_out0[:ATTN_WIDTH][q_cols])
    wof = bf(w_out0[ATTN_WIDTH:])
    attn_g = vec(attn_out_g[0][q_cols])
    qg = vec(jnp.tile(q_norm_g[0], 2))
    kg = vec(jnp.tile(k_norm_g[0], 2))

    cos_np, sin_np = _rope_tables(S)
    front_w = (vec(ln_emb_g), vec(ln_emb_b), bf(ff1_gate[0]), bf(ff1_up[0]), bf(ff1_down[0]),
               vec(ln1_g[0]), vec(ln1_b[0]), win, qg, kg)
    h1, q, k, v, u = _front(x.reshape(B * S, D), jnp.asarray(cos_np), jnp.asarray(sin_np), *front_w,
                            tm=TM, name="front")
    _, _, km, vm, um = _front(meta_tokens.astype(F32), jnp.ones((N_META, LANES), F32),
                              jnp.zeros((N_META, LANES), F32), *front_w, tm=N_META, name="front_meta")
    pad_rows = ((0, LANES - N_META), (0, 0))
    km, vm, um = jnp.pad(km, pad_rows), jnp.pad(vm, pad_rows), jnp.pad(um, pad_rows)

    attn = _attention(q, k, v, km, vm, attn_g, batch=B, seq=S)

    ct, st, cm, sm = (bf(jnp.asarray(t)) for t in _dft_tables(S))
    c64, s64 = (bf(jnp.asarray(t)) for t in _channel_dft_tables())
    four = _fourier(ct, st, cm, sm, u, um, c64, s64, vec(fourier_out_g[0]), batch=B, seq=S)

    out = _back(attn, four, h1, woa, wof, vec(ln2_g[0]), vec(ln2_b[0]),
                bf(ff2_gate[0]), bf(ff2_up[0]), bf(ff2_down[0]), vec(ln3_g[0]), vec(ln3_b[0]))
    return out.reshape(B, S, D)
```

```python
import functools
import math

import numpy as np
import jax
import jax.numpy as jnp
from jax import lax
from jax.experimental import pallas as pl
from jax.experimental.pallas import tpu as pltpu

F32 = jnp.float32
BF16 = jnp.bfloat16

D_MODEL = 1024
N_META = 16
GRID_W = 64
HEAD_DIM = 64
ATTN_WIDTH = 512
KV_WIDTH = 128
FOURIER_WIDTH = 512
FOURIER_GROUP_DIM = 64
D_FF = 2816
ROPE_THETA = 10000.0
RMS_EPS = 1e-6
LN_EPS = 1e-5
DEPTH = 1
ALPHA = (2.0 * DEPTH) ** 0.25

LANES = 128
VMEM_LIMIT = 56 * 1024 * 1024

TM = 512
TQ = 512
ATTN_TILES_PER_STEP = 4
TR = 1024
FF_CHUNKS = ((0, 1024), (1024, 2048), (2048, 2816))
MASKED = -1e30
Q_SCALE = HEAD_DIM ** -0.5 * math.log2(math.e)
FIXED_SHIFT_LIMIT = 48.0


def _dot(a, b):
    return jnp.dot(a, b, preferred_element_type=F32)


def _dot_nt(a, b):
    return lax.dot_general(a, b, (((1,), (1,)), ((), ())), preferred_element_type=F32)


def _layer_norm(x, g, b):
    mu = jnp.mean(x, axis=-1, keepdims=True)
    xc = x - mu
    var = jnp.mean(xc * xc, axis=-1, keepdims=True)
    return xc * lax.rsqrt(var + LN_EPS) * g + b


def _swiglu(hb, wg_ref, wu_ref, wd_ref):
    acc = None
    for c0, c1 in FF_CHUNKS:
        g = _dot(hb, wg_ref[:, c0:c1])
        u = _dot(hb, wu_ref[:, c0:c1])
        hid = (g * jax.nn.sigmoid(g) * u).astype(BF16)
        d = _dot(hid, wd_ref[c0:c1, :])
        acc = d if acc is None else acc + d
    return acc


def _front_body(x_ref, cos_ref, sin_ref, lneg_ref, lneb_ref, wg_ref, wu_ref, wd_ref,
                l1g_ref, l1b_ref, win_ref, qg_ref, kg_ref,
                h1_ref, q_ref, k_ref, v_ref, u_ref):
    h0 = _layer_norm(x_ref[...], lneg_ref[...], lneb_ref[...])
    f = _swiglu(h0.astype(BF16), wg_ref, wu_ref, wd_ref)
    h1 = _layer_norm(ALPHA * h0 + 0.5 * f, l1g_ref[...], l1b_ref[...])
    h1_ref[...] = h1
    z = _dot(h1.astype(BF16), win_ref[...])

    cos = cos_ref[...]
    sin = sin_ref[...]
    lane = lax.broadcasted_iota(jnp.int32, (1, LANES), 1)
    lo = lane < HEAD_DIM
    even = (lane & 1) == 0

    def norm_rope(zc, g):
        sq = zc * zc
        s_lo = jnp.sum(jnp.where(lo, sq, 0.0), axis=-1, keepdims=True)
        s_hi = jnp.sum(jnp.where(lo, 0.0, sq), axis=-1, keepdims=True)
        ms = jnp.where(lo, s_lo, s_hi) * (1.0 / HEAD_DIM)
        y = zc * lax.rsqrt(ms + RMS_EPS) * g
        nxt = pltpu.roll(y, LANES - 1, 1)
        prv = pltpu.roll(y, 1, 1)
        return y * cos + jnp.where(even, nxt, prv) * sin

    qg = qg_ref[...]
    for p in range(ATTN_WIDTH // LANES):
        c = slice(p * LANES, (p + 1) * LANES)
        q_ref[:, c] = (norm_rope(z[:, c], qg) * Q_SCALE).astype(BF16)
    k_ref[...] = norm_rope(z[:, ATTN_WIDTH:ATTN_WIDTH + KV_WIDTH], kg_ref[...]).astype(BF16)
    v_ref[...] = z[:, ATTN_WIDTH + KV_WIDTH:ATTN_WIDTH + 2 * KV_WIDTH].astype(BF16)
    u_ref[...] = z[:, ATTN_WIDTH + 2 * KV_WIDTH:].astype(BF16)


def _const_spec(shape):
    return pl.BlockSpec(shape, lambda *_: (0,) * len(shape), pipeline_mode=pl.Buffered(1))


def _front(x2, cos_t, sin_t, lneg, lneb, wg, wu, wd, l1g, l1b, win, qg, kg, *, tm, name):
    n = x2.shape[0]
    n_pos = cos_t.shape[0] // tm
    in_w = win.shape[1]
    row = lambda w: pl.BlockSpec((tm, w), lambda i: (i, 0))
    pos = pl.BlockSpec((tm, LANES), lambda i: (i % n_pos, 0))
    out_shape = (
        jax.ShapeDtypeStruct((n, D_MODEL), F32),
        jax.ShapeDtypeStruct((n, ATTN_WIDTH), BF16),
        jax.ShapeDtypeStruct((n, KV_WIDTH), BF16),
        jax.ShapeDtypeStruct((n, KV_WIDTH), BF16),
        jax.ShapeDtypeStruct((n, FOURIER_WIDTH), BF16),
    )
    return pl.pallas_call(
        _front_body,
        out_shape=out_shape,
        grid=(n // tm,),
        in_specs=[row(D_MODEL), pos, pos,
                  _const_spec((1, D_MODEL)), _const_spec((1, D_MODEL)),
                  _const_spec((D_MODEL, D_FF)), _const_spec((D_MODEL, D_FF)), _const_spec((D_FF, D_MODEL)),
                  _const_spec((1, D_MODEL)), _const_spec((1, D_MODEL)),
                  _const_spec((D_MODEL, in_w)),
                  _const_spec((1, LANES)), _const_spec((1, LANES))],
        out_specs=(row(D_MODEL), row(ATTN_WIDTH), row(KV_WIDTH), row(KV_WIDTH), row(FOURIER_WIDTH)),
        compiler_params=pltpu.CompilerParams(dimension_semantics=("arbitrary",),
                                             vmem_limit_bytes=VMEM_LIMIT),
        name=name,
    )(x2, cos_t, sin_t, lneg, lneb, wg, wu, wd, l1g, l1b, win, qg, kg)


def _attn_body(bound_ref, q_ref, k_ref, v_ref, km_ref, vm_ref, g_ref, o_ref, *, use_bound):
    lane = lax.broadcasted_iota(jnp.int32, (1, LANES), 1)
    lo = lane < HEAD_DIM
    k = k_ref[...]
    km = km_ref[...]
    v = v_ref[...]
    vm = vm_ref[...]
    zk = jnp.zeros_like(k)
    zm = jnp.zeros_like(km)
    k_sel = (jnp.where(lo, k, zk), jnp.where(lo, zk, k))
    km_sel = (jnp.where(lo, km, zm), jnp.where(lo, zm, km))
    v_one = (jnp.where(lo, v, jnp.ones_like(v)), jnp.where(lo, jnp.ones_like(v), v))
    vm_one = (jnp.where(lo, vm, jnp.ones_like(vm)), jnp.where(lo, jnp.ones_like(vm), vm))
    meta_bias = jnp.where(lane < N_META, 0.0, MASKED).astype(F32)
    bound = bound_ref[0, 0]

    for t in range(q_ref.shape[0] // TQ):
        rows = slice(t * TQ, (t + 1) * TQ)
        outs = []
        ssq = None
        for p in range(ATTN_WIDTH // LANES):
            q2 = q_ref[rows, p * LANES:(p + 1) * LANES]
            per_group = []
            for g in range(2):
                s = _dot_nt(q2, k_sel[g])
                sm = _dot_nt(q2, km_sel[g]) + meta_bias
                if use_bound:
                    m = bound
                else:
                    m = jnp.maximum(jnp.max(s, axis=-1, keepdims=True), jnp.max(sm, axis=-1, keepdims=True))
                e = jnp.exp2(s - m).astype(BF16)
                em = jnp.exp2(sm - m).astype(BF16)
                per_group.append(_dot(e, v_one[g]) + _dot(em, vm_one[g]))
            num = jnp.where(lo, per_group[0], per_group[1])
            den = pltpu.roll(jnp.where(lo, per_group[1], per_group[0]), HEAD_DIM, 1)
            op = num / den
            sq = jnp.sum(op * op, axis=-1, keepdims=True)
            ssq = sq if ssq is None else ssq + sq
            outs.append(op)
        rinv = lax.rsqrt(ssq * (1.0 / ATTN_WIDTH) + RMS_EPS)
        for p, op in enumerate(outs):
            c = slice(p * LANES, (p + 1) * LANES)
            o_ref[rows, c] = (op * rinv * g_ref[:, c]).astype(BF16)


def _attention(bound, q, k, v, km, vm, g, *, batch, seq, use_bound):
    rows = TQ * (ATTN_TILES_PER_STEP if use_bound else 1)
    nq = seq // rows
    return pl.pallas_call(
        functools.partial(_attn_body, use_bound=use_bound),
        out_shape=jax.ShapeDtypeStruct((batch * seq, ATTN_WIDTH), BF16),
        grid=(batch, nq),
        in_specs=[pl.BlockSpec(memory_space=pltpu.SMEM),
                  pl.BlockSpec((rows, ATTN_WIDTH), lambda b, i: (b * nq + i, 0)),
                  pl.BlockSpec((seq, KV_WIDTH), lambda b, i: (b, 0)),
                  pl.BlockSpec((seq, KV_WIDTH), lambda b, i: (b, 0)),
                  _const_spec((LANES, KV_WIDTH)), _const_spec((LANES, KV_WIDTH)),
                  _const_spec((1, ATTN_WIDTH))],
        out_specs=pl.BlockSpec((rows, ATTN_WIDTH), lambda b, i: (b * nq + i, 0)),
        compiler_params=pltpu.CompilerParams(dimension_semantics=("arbitrary", "arbitrary"),
                                             vmem_limit_bytes=VMEM_LIMIT),
        name="attention_bound_shift" if use_bound else "attention_rowmax_shift",
    )(bound, q, k, v, km, vm, g)


def _fourier_body(ct_ref, st_ref, cm_ref, sm_ref, u_ref, um_ref, c64_ref, s64_ref, g_ref,
                  o_ref, ymeta_ref):
    @pl.when(pl.program_id(1) == 0)
    def _():
        um = um_ref[...]
        tcm = _dot(cm_ref[...], um).astype(BF16)
        tsm = _dot(sm_ref[...], um).astype(BF16)
        ymeta_ref[...] = _dot(tcm, c64_ref[...]) + _dot(tsm, s64_ref[...])

    u = u_ref[...]
    tc = _dot(ct_ref[...], u).astype(BF16)
    ts = _dot(st_ref[...], u).astype(BF16)
    y = _dot(tc, c64_ref[...]) + _dot(ts, s64_ref[...]) + ymeta_ref[...]
    rinv = lax.rsqrt(jnp.mean(y * y, axis=-1, keepdims=True) + RMS_EPS)
    o_ref[...] = (y * rinv * g_ref[...]).astype(BF16)


def _fourier(ct, st, cm, sm, u, um, c64, s64, g, *, batch, seq):
    nr = seq // TR
    return pl.pallas_call(
        _fourier_body,
        out_shape=jax.ShapeDtypeStruct((batch * seq, FOURIER_WIDTH), BF16),
        grid=(nr, batch),
        in_specs=[pl.BlockSpec((TR, seq), lambda r, b: (r, 0)),
                  pl.BlockSpec((TR, seq), lambda r, b: (r, 0)),
                  pl.BlockSpec((TR, LANES), lambda r, b: (r, 0)),
                  pl.BlockSpec((TR, LANES), lambda r, b: (r, 0)),
                  pl.BlockSpec((seq, FOURIER_WIDTH), lambda r, b: (b, 0)),
                  _const_spec((LANES, FOURIER_WIDTH)),
                  _const_spec((FOURIER_WIDTH, FOURIER_WIDTH)), _const_spec((FOURIER_WIDTH, FOURIER_WIDTH)),
                  _const_spec((1, FOURIER_WIDTH))],
        out_specs=pl.BlockSpec((TR, FOURIER_WIDTH), lambda r, b: (b * nr + r, 0)),
        scratch_shapes=[pltpu.VMEM((TR, FOURIER_WIDTH), F32)],
        compiler_params=pltpu.CompilerParams(dimension_semantics=("arbitrary", "arbitrary"),
                                             vmem_limit_bytes=VMEM_LIMIT),
        name="fourier",
    )(ct, st, cm, sm, u, um, c64, s64, g)


def _back_body(a_ref, f_ref, h1_ref, woa_ref, wof_ref, l2g_ref, l2b_ref, wg_ref, wu_ref, wd_ref,
               l3g_ref, l3b_ref, o_ref):
    mix = _dot(a_ref[...], woa_ref[...]) + _dot(f_ref[...], wof_ref[...])
    h2 = _layer_norm(ALPHA * h1_ref[...] + mix, l2g_ref[...], l2b_ref[...])
    f = _swiglu(h2.astype(BF16), wg_ref, wu_ref, wd_ref)
    o_ref[...] = _layer_norm(ALPHA * h2 + 0.5 * f, l3g_ref[...], l3b_ref[...])


def _back(a, f, h1, woa, wof, l2g, l2b, wg, wu, wd, l3g, l3b):
    n = h1.shape[0]
    row = lambda w: pl.BlockSpec((TM, w), lambda i: (i, 0))
    return pl.pallas_call(
        _back_body,
        out_shape=jax.ShapeDtypeStruct((n, D_MODEL), F32),
        grid=(n // TM,),
        in_specs=[row(ATTN_WIDTH), row(FOURIER_WIDTH), row(D_MODEL),
                  _const_spec((ATTN_WIDTH, D_MODEL)), _const_spec((FOURIER_WIDTH, D_MODEL)),
                  _const_spec((1, D_MODEL)), _const_spec((1, D_MODEL)),
                  _const_spec((D_MODEL, D_FF)), _const_spec((D_MODEL, D_FF)), _const_spec((D_FF, D_MODEL)),
                  _const_spec((1, D_MODEL)), _const_spec((1, D_MODEL))],
        out_specs=row(D_MODEL),
        compiler_params=pltpu.CompilerParams(dimension_semantics=("arbitrary",),
                                             vmem_limit_bytes=VMEM_LIMIT),
        name="back",
    )(a, f, h1, woa, wof, l2g, l2b, wg, wu, wd, l3g, l3b)


@functools.lru_cache(maxsize=None)
def _rope_tables(seq):
    t = np.arange(seq)
    row = (t // GRID_W).astype(np.float64)
    col = (t % GRID_W).astype(np.float64)
    n_axis = HEAD_DIM // 4
    inv_freq = ROPE_THETA ** (-np.arange(n_axis, dtype=np.float64) / n_axis)
    ang = np.concatenate([row[:, None] * inv_freq, col[:, None] * inv_freq], axis=-1)
    cos = np.repeat(np.cos(ang), 2, axis=-1)
    sin = np.repeat(np.sin(ang), 2, axis=-1) * np.tile([-1.0, 1.0], HEAD_DIM // 2)
    return (np.tile(cos, (1, 2)).astype(np.float32), np.tile(sin, (1, 2)).astype(np.float32))


@functools.lru_cache(maxsize=None)
def _dft_tables(seq):
    L = seq + N_META
    m = np.arange(N_META, L, dtype=np.int64)[:, None]
    l = np.arange(L, dtype=np.int64)[None, :]
    ang = (2.0 * np.pi / L) * ((m * l) % L).astype(np.float64)
    c = (np.cos(ang) / math.sqrt(L)).astype(np.float32)
    s = (-np.sin(ang) / math.sqrt(L)).astype(np.float32)
    pad = ((0, 0), (0, LANES - N_META))
    return (c[:, N_META:], s[:, N_META:], np.pad(c[:, :N_META], pad), np.pad(s[:, :N_META], pad))


@functools.lru_cache(maxsize=None)
def _channel_dft_tables():
    n = FOURIER_GROUP_DIM
    idx = np.arange(n, dtype=np.int64)
    ang = (2.0 * np.pi / n) * ((idx[:, None] * idx[None, :]) % n).astype(np.float64)
    eye = np.eye(FOURIER_WIDTH // n)
    c = np.kron(eye, np.cos(ang) / math.sqrt(n)).astype(np.float32)
    s = np.kron(eye, np.sin(ang) / math.sqrt(n)).astype(np.float32)
    return c, s


def kernel(x, meta_tokens, ln_emb_g, ln_emb_b, ff1_gate, ff1_up, ff1_down, ln1_g, ln1_b, w_in, q_norm_g,
           k_norm_g, attn_out_g, fourier_out_g, w_out, ln2_g, ln2_b, ff2_gate, ff2_up, ff2_down, ln3_g, ln3_b):
    B, S, D = x.shape
    assert D == D_MODEL and S % TM == 0 and S % (TQ * ATTN_TILES_PER_STEP) == 0 and S % TR == 0
    assert w_in.shape[0] == DEPTH == 1
    vec = lambda a: a.reshape(1, -1).astype(F32)
    bf = lambda a: a.astype(BF16)

    n_rep = (ATTN_WIDTH // HEAD_DIM) // 2
    head_order = np.stack([np.arange(n_rep), np.arange(n_rep) + n_rep], axis=1).reshape(-1)
    q_cols = (head_order[:, None] * HEAD_DIM + np.arange(HEAD_DIM)[None, :]).reshape(-1)
    w_in0 = w_in[0]
    win = bf(jnp.concatenate([w_in0[:, q_cols], w_in0[:, ATTN_WIDTH:]], axis=1))
    w_out0 = w_out[0]
    woa = bf(w_out0[:ATTN_WIDTH][q_cols])
    wof = bf(w_out0[ATTN_WIDTH:])
    attn_g = vec(attn_out_g[0][q_cols])
    qg = vec(jnp.tile(q_norm_g[0], 2))
    kg = vec(jnp.tile(k_norm_g[0], 2))

    cos_np, sin_np = _rope_tables(S)
    front_w = (vec(ln_emb_g), vec(ln_emb_b), bf(ff1_gate[0]), bf(ff1_up[0]), bf(ff1_down[0]),
               vec(ln1_g[0]), vec(ln1_b[0]), win, qg, kg)
    h1, q, k, v, u = _front(x.reshape(B * S, D), jnp.asarray(cos_np), jnp.asarray(sin_np), *front_w,
                            tm=TM, name="front")
    _, _, km, vm, um = _front(meta_tokens.astype(F32), jnp.ones((N_META, LANES), F32),
                              jnp.zeros((N_META, LANES), F32), *front_w, tm=N_META, name="front_meta")
    pad_rows = ((0, LANES - N_META), (0, 0))
    km, vm, um = jnp.pad(km, pad_rows), jnp.pad(vm, pad_rows), jnp.pad(um, pad_rows)

    score_bound = (HEAD_DIM * Q_SCALE * (1.0 + 2.0 ** -6)
                   * jnp.max(jnp.abs(q_norm_g[0])) * jnp.max(jnp.abs(k_norm_g[0])))
    score_bound = score_bound.astype(F32).reshape(1, 1)
    attend = functools.partial(_attention, score_bound, q, k, v, km, vm, attn_g, batch=B, seq=S)
    attn = lax.cond(score_bound[0, 0] <= FIXED_SHIFT_LIMIT,
                    functools.partial(attend, use_bound=True), functools.partial(attend, use_bound=False))

    ct, st, cm, sm = (bf(jnp.asarray(t)) for t in _dft_tables(S))
    c64, s64 = (bf(jnp.asarray(t)) for t in _channel_dft_tables())
    four = _fourier(ct, st, cm, sm, u, um, c64, s64, vec(fourier_out_g[0]), batch=B, seq=S)

    out = _back(attn, four, h1, woa, wof, vec(ln2_g[0]), vec(ln2_b[0]),
                bf(ff2_gate[0]), bf(ff2_up[0]), bf(ff2_down[0]), vec(ln3_g[0]), vec(ln3_b[0]))
    return out.reshape(B, S, D)
```

```python
import functools
import math

import numpy as np
import jax
import jax.numpy as jnp
from jax import lax
from jax.experimental import pallas as pl
from jax.experimental.pallas import tpu as pltpu

F32 = jnp.float32
BF16 = jnp.bfloat16

D_MODEL = 1024
N_META = 16
GRID_W = 64
HEAD_DIM = 64
ATTN_WIDTH = 512
KV_WIDTH = 128
FOURIER_WIDTH = 512
FOURIER_GROUP_DIM = 64
D_FF = 2816
ROPE_THETA = 10000.0
RMS_EPS = 1e-6
LN_EPS = 1e-5
DEPTH = 1
ALPHA = (2.0 * DEPTH) ** 0.25

LANES = 128
VMEM_LIMIT = 56 * 1024 * 1024

TM = 512
TQ = 512
ATTN_TILES_PER_STEP = 2
TR = 1024
FF_CHUNKS = ((0, 1024), (1024, 2048), (2048, 2816))
MASKED = -1e30
Q_SCALE = HEAD_DIM ** -0.5 * math.log2(math.e)
FIXED_SHIFT_LIMIT = 48.0


def _dot(a, b):
    return jnp.dot(a, b, preferred_element_type=F32)


def _layer_norm(x, g, b):
    mu = jnp.mean(x, axis=-1, keepdims=True)
    xc = x - mu
    var = jnp.mean(xc * xc, axis=-1, keepdims=True)
    return xc * lax.rsqrt(var + LN_EPS) * g + b


def _swiglu(hb, wg_ref, wu_ref, wd_ref):
    acc = None
    for c0, c1 in FF_CHUNKS:
        g = _dot(hb, wg_ref[:, c0:c1])
        u = _dot(hb, wu_ref[:, c0:c1])
        hid = (g * jax.nn.sigmoid(g) * u).astype(BF16)
        d = _dot(hid, wd_ref[c0:c1, :])
        acc = d if acc is None else acc + d
    return acc


def _front_body(x_ref, cos_ref, sin_ref, lneg_ref, lneb_ref, wg_ref, wu_ref, wd_ref,
                l1g_ref, l1b_ref, win_ref, qg_ref, kg_ref,
                h1_ref, q_ref, kt_ref, v_ref, u_ref):
    h0 = _layer_norm(x_ref[...], lneg_ref[...], lneb_ref[...])
    f = _swiglu(h0.astype(BF16), wg_ref, wu_ref, wd_ref)
    h1 = _layer_norm(ALPHA * h0 + 0.5 * f, l1g_ref[...], l1b_ref[...])
    h1_ref[...] = h1
    z = _dot(h1.astype(BF16), win_ref[...])

    cos = cos_ref[...]
    sin = sin_ref[...]
    lane = lax.broadcasted_iota(jnp.int32, (1, LANES), 1)
    lo = lane < HEAD_DIM
    even = (lane & 1) == 0

    def norm_rope(zc, g):
        sq = zc * zc
        s_lo = jnp.sum(jnp.where(lo, sq, 0.0), axis=-1, keepdims=True)
        s_hi = jnp.sum(jnp.where(lo, 0.0, sq), axis=-1, keepdims=True)
        ms = jnp.where(lo, s_lo, s_hi) * (1.0 / HEAD_DIM)
        y = zc * lax.rsqrt(ms + RMS_EPS) * g
        nxt = pltpu.roll(y, LANES - 1, 1)
        prv = pltpu.roll(y, 1, 1)
        return y * cos + jnp.where(even, nxt, prv) * sin

    qg = qg_ref[...]
    for p in range(ATTN_WIDTH // LANES):
        c = slice(p * LANES, (p + 1) * LANES)
        q_ref[:, c] = (norm_rope(z[:, c], qg) * Q_SCALE).astype(BF16)
    kt_ref[...] = norm_rope(z[:, ATTN_WIDTH:ATTN_WIDTH + KV_WIDTH], kg_ref[...]).T.astype(BF16)
    v_ref[...] = z[:, ATTN_WIDTH + KV_WIDTH:ATTN_WIDTH + 2 * KV_WIDTH].astype(BF16)
    u_ref[...] = z[:, ATTN_WIDTH + 2 * KV_WIDTH:].astype(BF16)


def _const_spec(shape):
    return pl.BlockSpec(shape, lambda *_: (0,) * len(shape), pipeline_mode=pl.Buffered(1))


def _front(x2, cos_t, sin_t, lneg, lneb, wg, wu, wd, l1g, l1b, win, qg, kg, *, tm, name):
    n = x2.shape[0]
    n_pos = cos_t.shape[0] // tm
    in_w = win.shape[1]
    row = lambda w: pl.BlockSpec((tm, w), lambda i: (i, 0))
    pos = pl.BlockSpec((tm, LANES), lambda i: (i % n_pos, 0))
    out_shape = (
        jax.ShapeDtypeStruct((n, D_MODEL), F32),
        jax.ShapeDtypeStruct((n, ATTN_WIDTH), BF16),
        jax.ShapeDtypeStruct((KV_WIDTH, n), BF16),
        jax.ShapeDtypeStruct((n, KV_WIDTH), BF16),
        jax.ShapeDtypeStruct((n, FOURIER_WIDTH), BF16),
    )
    return pl.pallas_call(
        _front_body,
        out_shape=out_shape,
        grid=(n // tm,),
        in_specs=[row(D_MODEL), pos, pos,
                  _const_spec((1, D_MODEL)), _const_spec((1, D_MODEL)),
                  _const_spec((D_MODEL, D_FF)), _const_spec((D_MODEL, D_FF)), _const_spec((D_FF, D_MODEL)),
                  _const_spec((1, D_MODEL)), _const_spec((1, D_MODEL)),
                  _const_spec((D_MODEL, in_w)),
                  _const_spec((1, LANES)), _const_spec((1, LANES))],
        out_specs=(row(D_MODEL), row(ATTN_WIDTH), pl.BlockSpec((KV_WIDTH, tm), lambda i: (0, i)),
                   row(KV_WIDTH), row(FOURIER_WIDTH)),
        compiler_params=pltpu.CompilerParams(dimension_semantics=("arbitrary",),
                                             vmem_limit_bytes=VMEM_LIMIT),
        name=name,
    )(x2, cos_t, sin_t, lneg, lneb, wg, wu, wd, l1g, l1b, win, qg, kg)


def _attn_body(bound_ref, q_ref, kt_ref, v_ref, kmt_ref, vm_ref, g_ref, o_ref, *, use_bound):
    lane = lax.broadcasted_iota(jnp.int32, (1, LANES), 1)
    lo = lane < HEAD_DIM
    kt = kt_ref[...]
    kmt = kmt_ref[...]
    v = v_ref[...]
    vm = vm_ref[...]
    zk = jnp.zeros_like(kt)
    zm = jnp.zeros_like(kmt)
    head0 = lax.broadcasted_iota(jnp.int32, (KV_WIDTH, 1), 0) < HEAD_DIM
    k_sel = (jnp.where(head0, kt, zk), jnp.where(head0, zk, kt))
    km_sel = (jnp.where(head0, kmt, zm), jnp.where(head0, zm, kmt))
    v_one = (jnp.where(lo, v, jnp.ones_like(v)), jnp.where(lo, jnp.ones_like(v), v))
    vm_one = (jnp.where(lo, vm, jnp.ones_like(vm)), jnp.where(lo, jnp.ones_like(vm), vm))
    meta_bias = jnp.where(lane < N_META, 0.0, MASKED).astype(F32)
    bound = bound_ref[0, 0]

    for t in range(q_ref.shape[0] // TQ):
        rows = slice(t * TQ, (t + 1) * TQ)
        outs = []
        ssq = None
        for p in range(ATTN_WIDTH // LANES):
            q2 = q_ref[rows, p * LANES:(p + 1) * LANES]
            per_group = []
            for g in range(2):
                s = _dot(q2, k_sel[g])
                sm = _dot(q2, km_sel[g]) + meta_bias
                if use_bound:
                    m = bound
                else:
                    m = jnp.maximum(jnp.max(s, axis=-1, keepdims=True), jnp.max(sm, axis=-1, keepdims=True))
                e = jnp.exp2(s - m).astype(BF16)
                em = jnp.exp2(sm - m).astype(BF16)
                per_group.append(_dot(e, v_one[g]) + _dot(em, vm_one[g]))
            num = jnp.where(lo, per_group[0], per_group[1])
            den = pltpu.roll(jnp.where(lo, per_group[1], per_group[0]), HEAD_DIM, 1)
            op = num / den
            sq = jnp.sum(op * op, axis=-1, keepdims=True)
            ssq = sq if ssq is None else ssq + sq
            outs.append(op)
        rinv = lax.rsqrt(ssq * (1.0 / ATTN_WIDTH) + RMS_EPS)
        for p, op in enumerate(outs):
            c = slice(p * LANES, (p + 1) * LANES)
            o_ref[rows, c] = (op * rinv * g_ref[:, c]).astype(BF16)


def _attention(bound, q, kt, v, kmt, vm, g, *, batch, seq, use_bound):
    rows = TQ * (ATTN_TILES_PER_STEP if use_bound else 1)
    nq = seq // rows
    return pl.pallas_call(
        functools.partial(_attn_body, use_bound=use_bound),
        out_shape=jax.ShapeDtypeStruct((batch * seq, ATTN_WIDTH), BF16),
        grid=(batch, nq),
        in_specs=[pl.BlockSpec(memory_space=pltpu.SMEM),
                  pl.BlockSpec((rows, ATTN_WIDTH), lambda b, i: (b * nq + i, 0)),
                  pl.BlockSpec((KV_WIDTH, seq), lambda b, i: (0, b)),
                  pl.BlockSpec((seq, KV_WIDTH), lambda b, i: (b, 0)),
                  _const_spec((KV_WIDTH, LANES)), _const_spec((LANES, KV_WIDTH)),
                  _const_spec((1, ATTN_WIDTH))],
        out_specs=pl.BlockSpec((rows, ATTN_WIDTH), lambda b, i: (b * nq + i, 0)),
        compiler_params=pltpu.CompilerParams(dimension_semantics=("arbitrary", "arbitrary"),
                                             vmem_limit_bytes=VMEM_LIMIT),
        name="attention_bound_shift" if use_bound else "attention_rowmax_shift",
    )(bound, q, kt, v, kmt, vm, g)


def _fourier_body(ct_ref, st_ref, cm_ref, sm_ref, u_ref, um_ref, c64_ref, s64_ref, g_ref,
                  o_ref, ymeta_ref):
    @pl.when(pl.program_id(1) == 0)
    def _():
        um = um_ref[...]
        tcm = _dot(cm_ref[...], um).astype(BF16)
        tsm = _dot(sm_ref[...], um).astype(BF16)
        ymeta_ref[...] = _dot(tcm, c64_ref[...]) + _dot(tsm, s64_ref[...])

    u = u_ref[...]
    tc = _dot(ct_ref[...], u).astype(BF16)
    ts = _dot(st_ref[...], u).astype(BF16)
    y = _dot(tc, c64_ref[...]) + _dot(ts, s64_ref[...]) + ymeta_ref[...]
    rinv = lax.rsqrt(jnp.mean(y * y, axis=-1, keepdims=True) + RMS_EPS)
    o_ref[...] = (y * rinv * g_ref[...]).astype(BF16)


def _fourier(ct, st, cm, sm, u, um, c64, s64, g, *, batch, seq):
    nr = seq // TR
    return pl.pallas_call(
        _fourier_body,
        out_shape=jax.ShapeDtypeStruct((batch * seq, FOURIER_WIDTH), BF16),
        grid=(nr, batch),
        in_specs=[pl.BlockSpec((TR, seq), lambda r, b: (r, 0)),
                  pl.BlockSpec((TR, seq), lambda r, b: (r, 0)),
                  pl.BlockSpec((TR, LANES), lambda r, b: (r, 0)),
                  pl.BlockSpec((TR, LANES), lambda r, b: (r, 0)),
                  pl.BlockSpec((seq, FOURIER_WIDTH), lambda r, b: (b, 0)),
                  _const_spec((LANES, FOURIER_WIDTH)),
                  _const_spec((FOURIER_WIDTH, FOURIER_WIDTH)), _const_spec((FOURIER_WIDTH, FOURIER_WIDTH)),
                  _const_spec((1, FOURIER_WIDTH))],
        out_specs=pl.BlockSpec((TR, FOURIER_WIDTH), lambda r, b: (b * nr + r, 0)),
        scratch_shapes=[pltpu.VMEM((TR, FOURIER_WIDTH), F32)],
        compiler_params=pltpu.CompilerParams(dimension_semantics=("arbitrary", "arbitrary"),
                                             vmem_limit_bytes=VMEM_LIMIT),
        name="fourier",
    )(ct, st, cm, sm, u, um, c64, s64, g)


def _back_body(a_ref, f_ref, h1_ref, woa_ref, wof_ref, l2g_ref, l2b_ref, wg_ref, wu_ref, wd_ref,
               l3g_ref, l3b_ref, o_ref):
    mix = _dot(a_ref[...], woa_ref[...]) + _dot(f_ref[...], wof_ref[...])
    h2 = _layer_norm(ALPHA * h1_ref[...] + mix, l2g_ref[...], l2b_ref[...])
    f = _swiglu(h2.astype(BF16), wg_ref, wu_ref, wd_ref)
    o_ref[...] = _layer_norm(ALPHA * h2 + 0.5 * f, l3g_ref[...], l3b_ref[...])


def _back(a, f, h1, woa, wof, l2g, l2b, wg, wu, wd, l3g, l3b):
    n = h1.shape[0]
    row = lambda w: pl.BlockSpec((TM, w), lambda i: (i, 0))
    return pl.pallas_call(
        _back_body,
        out_shape=jax.ShapeDtypeStruct((n, D_MODEL), F32),
        grid=(n // TM,),
        in_specs=[row(ATTN_WIDTH), row(FOURIER_WIDTH), row(D_MODEL),
                  _const_spec((ATTN_WIDTH, D_MODEL)), _const_spec((FOURIER_WIDTH, D_MODEL)),
                  _const_spec((1, D_MODEL)), _const_spec((1, D_MODEL)),
                  _const_spec((D_MODEL, D_FF)), _const_spec((D_MODEL, D_FF)), _const_spec((D_FF, D_MODEL)),
                  _const_spec((1, D_MODEL)), _const_spec((1, D_MODEL))],
        out_specs=row(D_MODEL),
        compiler_params=pltpu.CompilerParams(dimension_semantics=("arbitrary",),
                                             vmem_limit_bytes=VMEM_LIMIT),
        name="back",
    )(a, f, h1, woa, wof, l2g, l2b, wg, wu, wd, l3g, l3b)


@functools.lru_cache(maxsize=None)
def _rope_tables(seq):
    t = np.arange(seq)
    row = (t // GRID_W).astype(np.float64)
    col = (t % GRID_W).astype(np.float64)
    n_axis = HEAD_DIM // 4
    inv_freq = ROPE_THETA ** (-np.arange(n_axis, dtype=np.float64) / n_axis)
    ang = np.concatenate([row[:, None] * inv_freq, col[:, None] * inv_freq], axis=-1)
    cos = np.repeat(np.cos(ang), 2, axis=-1)
    sin = np.repeat(np.sin(ang), 2, axis=-1) * np.tile([-1.0, 1.0], HEAD_DIM // 2)
    return (np.tile(cos, (1, 2)).astype(np.float32), np.tile(sin, (1, 2)).astype(np.float32))


@functools.lru_cache(maxsize=None)
def _dft_tables(seq):
    L = seq + N_META
    m = np.arange(N_META, L, dtype=np.int64)[:, None]
    l = np.arange(L, dtype=np.int64)[None, :]
    ang = (2.0 * np.pi / L) * ((m * l) % L).astype(np.float64)
    c = (np.cos(ang) / math.sqrt(L)).astype(np.float32)
    s = (-np.sin(ang) / math.sqrt(L)).astype(np.float32)
    pad = ((0, 0), (0, LANES - N_META))
    return (c[:, N_META:], s[:, N_META:], np.pad(c[:, :N_META], pad), np.pad(s[:, :N_META], pad))


@functools.lru_cache(maxsize=None)
def _channel_dft_tables():
    n = FOURIER_GROUP_DIM
    idx = np.arange(n, dtype=np.int64)
    ang = (2.0 * np.pi / n) * ((idx[:, None] * idx[None, :]) % n).astype(np.float64)
    eye = np.eye(FOURIER_WIDTH // n)
    c = np.kron(eye, np.cos(ang) / math.sqrt(n)).astype(np.float32)
    s = np.kron(eye, np.sin(ang) / math.sqrt(n)).astype(np.float32)
    return c, s


def kernel(x, meta_tokens, ln_emb_g, ln_emb_b, ff1_gate, ff1_up, ff1_down, ln1_g, ln1_b, w_in, q_norm_g,
           k_norm_g, attn_out_g, fourier_out_g, w_out, ln2_g, ln2_b, ff2_gate, ff2_up, ff2_down, ln3_g, ln3_b):
    B, S, D = x.shape
    assert D == D_MODEL and S % TM == 0 and S % (TQ * ATTN_TILES_PER_STEP) == 0 and S % TR == 0
    assert w_in.shape[0] == DEPTH == 1
    vec = lambda a: a.reshape(1, -1).astype(F32)
    bf = lambda a: a.astype(BF16)

    n_rep = (ATTN_WIDTH // HEAD_DIM) // 2
    head_order = np.stack([np.arange(n_rep), np.arange(n_rep) + n_rep], axis=1).reshape(-1)
    q_cols = (head_order[:, None] * HEAD_DIM + np.arange(HEAD_DIM)[None, :]).reshape(-1)
    w_in0 = w_in[0]
    win = bf(jnp.concatenate([w_in0[:, q_cols], w_in0[:, ATTN_WIDTH:]], axis=1))
    w_out0 = w_out[0]
    woa = bf(w_out0[:ATTN_WIDTH][q_cols])
    wof = bf(w_out0[ATTN_WIDTH:])
    attn_g = vec(attn_out_g[0][q_cols])
    qg = vec(jnp.tile(q_norm_g[0], 2))
    kg = vec(jnp.tile(k_norm_g[0], 2))

    cos_np, sin_np = _rope_tables(S)
    front_w = (vec(ln_emb_g), vec(ln_emb_b), bf(ff1_gate[0]), bf(ff1_up[0]), bf(ff1_down[0]),
               vec(ln1_g[0]), vec(ln1_b[0]), win, qg, kg)
    h1, q, kt, v, u = _front(x.reshape(B * S, D), jnp.asarray(cos_np), jnp.asarray(sin_np), *front_w,
                             tm=TM, name="front")
    _, _, kmt, vm, um = _front(meta_tokens.astype(F32), jnp.ones((N_META, LANES), F32),
                               jnp.zeros((N_META, LANES), F32), *front_w, tm=N_META, name="front_meta")
    pad_rows = ((0, LANES - N_META), (0, 0))
    vm, um = jnp.pad(vm, pad_rows), jnp.pad(um, pad_rows)
    kmt = jnp.pad(kmt, ((0, 0), (0, LANES - N_META)))

    score_bound = (HEAD_DIM * Q_SCALE * (1.0 + 2.0 ** -6)
                   * jnp.max(jnp.abs(q_norm_g[0])) * jnp.max(jnp.abs(k_norm_g[0])))
    score_bound = score_bound.astype(F32).reshape(1, 1)
    attend = functools.partial(_attention, score_bound, q, kt, v, kmt, vm, attn_g, batch=B, seq=S)
    attn = lax.cond(score_bound[0, 0] <= FIXED_SHIFT_LIMIT,
                    functools.partial(attend, use_bound=True), functools.partial(attend, use_bound=False))

    ct, st, cm, sm = (bf(jnp.asarray(t)) for t in _dft_tables(S))
    c64, s64 = (bf(jnp.asarray(t)) for t in _channel_dft_tables())
    four = _fourier(ct, st, cm, sm, u, um, c64, s64, vec(fourier_out_g[0]), batch=B, seq=S)

    out = _back(attn, four, h1, woa, wof, vec(ln2_g[0]), vec(ln2_b[0]),
                bf(ff2_gate[0]), bf(ff2_up[0]), bf(ff2_down[0]), vec(ln3_g[0]), vec(ln3_b[0]))
    return out.reshape(B, S, D)
```
